```python
import math
import jax
import jax.numpy as jnp
from jax import lax
import numpy as np

D_MODEL = 1024
BATCH = 32
SEQ = 2048
DEPTH = 2

CTX_LEN = 256
GRID_W = 64
NORM_EPS = 1e-6
N_MODS = 6
N_BRANCH = 3
BRANCH_W = D_MODEL
LRU_WIDTH = BRANCH_W
LRU_BLOCKS = 8
LRU_BLOCK = LRU_WIDTH // LRU_BLOCKS
LRU_C = 8.0
CONV_WIDTH = 4
CONV_PAD = (2, 1)
ML_HEADS = 4
ML_DV = BRANCH_W // ML_HEADS
ML_DQK = ML_DV // 2
ML_CHUNK = 64
ML_GATES = 2 * 2 * ML_HEADS
DA_HEADS = 8
DA_DV = BRANCH_W // DA_HEADS
DA_DH = DA_DV // 2
ROPE_THETA = 10000.0
Q_BLOCK = 128
N_EXPERTS = 64
TOP_K = 6
EXPERT_FF = D_MODEL // 4
SHARED_FF = D_MODEL // 4
ROUTED_SCALE = 2.5
EXPERT_GROUP = 8
PROJ_SIZES = (LRU_WIDTH, LRU_WIDTH,
              ML_HEADS * ML_DQK, ML_HEADS * ML_DQK, ML_HEADS * ML_DV, ML_HEADS * ML_DV, ML_GATES,
              DA_HEADS * 2 * DA_DH, DA_HEADS * 2 * DA_DH, DA_HEADS * DA_DV,
              N_BRANCH * D_MODEL)
PROJ_WIDTH = sum(PROJ_SIZES)

kernel_name = 'hybrid_rglru_mlstm_diffattn_moe_dit'


def rms_norm(x, g):
    xf = x.astype(jnp.float32)
    y = xf * lax.rsqrt(jnp.mean(xf * xf, axis=-1, keepdims=True) + NORM_EPS)
    return (y * g.astype(jnp.float32)).astype(x.dtype)


def modulate(h, shift, scale):
    return h * (1 + scale) + shift


def ada_mods(cond, w_ada, b_ada):
    m = jax.nn.silu(cond) @ w_ada + b_ada
    return jnp.split(m, N_MODS, axis=-1)


def flip_seq(t):
    return t[:, ::-1]


def no_flip(t):
    return t


def proj_split_points():
    return [int(v) for v in np.cumsum(PROJ_SIZES)[:-1]]


def short_conv(x, w, b):
    y = lax.conv_general_dilated(x, w.astype(x.dtype)[:, None, :], window_strides=(1,),
                                 padding=[CONV_PAD], dimension_numbers=('NWC', 'WIO', 'NWC'),
                                 feature_group_count=x.shape[-1])
    return y + b.astype(x.dtype)


def rglru_coeffs(u, wa, ba, wx, bx, lam):
    uf = u.astype(jnp.float32)
    ub = uf.reshape(*uf.shape[:-1], LRU_BLOCKS, LRU_BLOCK)
    r = jax.nn.sigmoid(jnp.einsum('bshi,hij->bshj', ub, wa.astype(jnp.float32)).reshape(uf.shape) + ba.astype(jnp.float32))
    i = jax.nn.sigmoid(jnp.einsum('bshi,hij->bshj', ub, wx.astype(jnp.float32)).reshape(uf.shape) + bx.astype(jnp.float32))
    log_a = -LRU_C * r * jax.nn.softplus(-lam.astype(jnp.float32))
    a = jnp.exp(log_a)
    b = jnp.sqrt(-jnp.expm1(2.0 * log_a)) * (i * uf)
    return a, b


def linear_recurrence(a, b, h0):
    def combine(l, r):
        return l[0] * r[0], r[0] * l[1] + r[1]
    a_cum, h = lax.associative_scan(combine, (a, b), axis=1)
    return h + a_cum * h0[:, None, :]


def rglru_branch(ctx_in, lat_in, conv_w, conv_b, wa, ba, wx, bx, lam, need_ctx):
    xa_c, ga_c = ctx_in
    xa_l, ga_l = lat_in
    u_c = short_conv(xa_c, conv_w, conv_b)
    u_l = short_conv(xa_l, conv_w, conv_b)
    h_zero = jnp.zeros((u_l.shape[0], LRU_WIDTH), jnp.float32)
    h_l = 0.0
    h_c = 0.0
    for d in range(2):
        fl = flip_seq if d else no_flip
        a_c, b_c = rglru_coeffs(fl(u_c), wa[d], ba[d], wx[d], bx[d], lam[d])
        hs_c = linear_recurrence(a_c, b_c, h_zero)
        a_l, b_l = rglru_coeffs(fl(u_l), wa[d], ba[d], wx[d], bx[d], lam[d])
        hs_l = linear_recurrence(a_l, b_l, hs_c[:, -1])
        h_l = h_l + fl(hs_l)
        if need_ctx:
            h_c = h_c + fl(hs_c)
    y_l = (jax.nn.gelu(ga_l) * h_l).astype(ga_l.dtype)
    y_c = (jax.nn.gelu(ga_c) * h_c).astype(ga_c.dtype) if need_ctx else None
    return y_l, y_c


def mlstm_heads(q, k, v, g, gate_b):
    B, S = q.shape[:2]
    q = q.reshape(B, S, ML_HEADS, ML_DQK)
    k = k.reshape(B, S, ML_HEADS, ML_DQK) * (ML_DQK ** -0.5)
    v = v.reshape(B, S, ML_HEADS, ML_DV)
    g = g.astype(jnp.float32).reshape(B, S, 2, 2, ML_HEADS) + gate_b.astype(jnp.float32)
    return q, k, v, g[:, :, :, 0], jax.nn.log_sigmoid(g[:, :, :, 1])


def mlstm_chunkwise(q, k, v, ig, lf, state):
    B, S, H, DK = q.shape
    DV = v.shape[-1]
    nc = S // ML_CHUNK
    causal = jnp.tril(jnp.ones((ML_CHUNK, ML_CHUNK), dtype=bool))

    def chunks(t):
        return t.reshape(B, nc, ML_CHUNK, *t.shape[2:]).swapaxes(0, 1)

    def step(carry, xs):
        C, n, m = carry
        qc, kc, vc, ic, fc = xs
        b = jnp.cumsum(fc, axis=1).swapaxes(1, 2)
        i_ = ic.swapaxes(1, 2)
        d_intra = jnp.where(causal, b[..., :, None] - b[..., None, :] + i_[..., None, :], -jnp.inf)
        d_inter = b + m[..., None]
        m_row = jnp.maximum(d_inter, d_intra.max(-1))
        w_intra = jnp.exp(d_intra - m_row[..., None])
        w_inter = jnp.exp(d_inter - m_row)
        s = jnp.einsum('bjhd,bshd->bhjs', qc, kc).astype(jnp.float32) * w_intra
        num = (jnp.einsum('bhjs,bshe->bjhe', s, vc)
               + jnp.einsum('bjhd,bhde->bjhe', qc, C) * w_inter.swapaxes(1, 2)[..., None])
        den = s.sum(-1) + jnp.einsum('bjhd,bhd->bhj', qc, n) * w_inter
        h = num / jnp.maximum(jnp.abs(den), jnp.exp(-m_row)).swapaxes(1, 2)[..., None]
        g_end = b[..., -1:] - b + i_
        m_new = jnp.maximum(b[..., -1] + m, g_end.max(-1))
        w_s = jnp.exp(g_end - m_new[..., None])
        w_c = jnp.exp(b[..., -1] + m - m_new)
        kw = kc * w_s.swapaxes(1, 2)[..., None]
        C_new = w_c[..., None, None] * C + jnp.einsum('bshd,bshe->bhde', kw, vc)
        n_new = w_c[..., None] * n + kw.sum(axis=1)
        return (C_new, n_new, m_new), h

    state, h = lax.scan(step, state, tuple(chunks(t) for t in (q, k, v, ig, lf)))
    return h.swapaxes(0, 1).reshape(B, S, H, DV), state


def mlstm_branch(ctx_in, lat_in, gate_b, norm_g, need_ctx):
    qc, kc, vc, ic, fc = mlstm_heads(ctx_in[0], ctx_in[1], ctx_in[2], ctx_in[4], gate_b)
    ql, kl, vl, il, flg = mlstm_heads(lat_in[0], lat_in[1], lat_in[2], lat_in[4], gate_b)
    B = ql.shape[0]
    zero = (jnp.zeros((B, ML_HEADS, ML_DQK, ML_DV), jnp.float32),
            jnp.zeros((B, ML_HEADS, ML_DQK), jnp.float32),
            jnp.zeros((B, ML_HEADS), jnp.float32))
    h_l = 0.0
    h_c = 0.0
    for d in range(2):
        fl = flip_seq if d else no_flip
        hs_c, st = mlstm_chunkwise(fl(qc), fl(kc), fl(vc), fl(ic[:, :, d]), fl(fc[:, :, d]), zero)
        hs_l, _ = mlstm_chunkwise(fl(ql), fl(kl), fl(vl), fl(il[:, :, d]), fl(flg[:, :, d]), st)
        h_l = h_l + fl(hs_l)
        if need_ctx:
            h_c = h_c + fl(hs_c)

    def finish(h, o):
        hn = rms_norm(h, norm_g).reshape(o.shape)
        return (hn * jax.nn.sigmoid(o.astype(jnp.float32))).astype(o.dtype)

    y_l = finish(h_l, lat_in[3])
    y_c = finish(h_c, ctx_in[3]) if need_ctx else None
    return y_l, y_c


def axial_rope_tables(rows):
    nf = DA_DH // 4
    freqs = jnp.power(ROPE_THETA, -jnp.arange(nf, dtype=jnp.float32) / nf)
    row = jnp.repeat(jnp.arange(rows, dtype=jnp.float32), GRID_W)
    col = jnp.tile(jnp.arange(GRID_W, dtype=jnp.float32), rows)
    ang = jnp.stack([row, col], axis=-1)[..., None] * freqs
    return jnp.cos(ang), jnp.sin(ang)


def apply_axial_rope(t, cos, sin):
    xs = t.reshape(*t.shape[:-1], 2, 2, DA_DH // 4)
    x1, x2 = xs[..., 0, :], xs[..., 1, :]
    c = cos[:, None, None].astype(t.dtype)
    s = sin[:, None, None].astype(t.dtype)
    return jnp.stack([x1 * c - x2 * s, x2 * c + x1 * s], axis=-2).reshape(t.shape)


def diff_softmax_attend(q, k, v, lam):
    s = jnp.einsum('bqhcd,bkhcd->bhcqk', q, k).astype(jnp.float32) * (DA_DH ** -0.5)
    p = jax.nn.softmax(s, axis=-1)
    a = p[:, :, 0] - lam * p[:, :, 1]
    return jnp.einsum('bhqk,bkhe->bqhe', a.astype(v.dtype), v)


def blocked_diff_attention(q, k, v, lam):
    B, S = q.shape[:2]
    nb = S // Q_BLOCK
    qb = q.reshape(B, nb, Q_BLOCK, *q.shape[2:]).swapaxes(0, 1)
    out = lax.map(lambda qi: diff_softmax_attend(qi, k, v, lam), qb)
    return out.swapaxes(0, 1).reshape(B, S, *out.shape[3:])


def diff_attn_branch(ctx_in, lat_in, qn_g, kn_g, lam_vec, subln_g, lam_init, cos, sin, need_ctx):
    def heads(q, k, v):
        B, S = q.shape[:2]
        q = rms_norm(q.reshape(B, S, DA_HEADS, 2, DA_DH), qn_g)
        k = rms_norm(k.reshape(B, S, DA_HEADS, 2, DA_DH), kn_g)
        return q, k, v.reshape(B, S, DA_HEADS, DA_DV)

    qc, kc, vc = heads(*ctx_in)
    ql, kl, vl = heads(*lat_in)
    ql = apply_axial_rope(ql, cos, sin)
    kl = apply_axial_rope(kl, cos, sin)
    lv = lam_vec.astype(jnp.float32)
    lam = jnp.exp(jnp.sum(lv[0] * lv[1])) - jnp.exp(jnp.sum(lv[2] * lv[3])) + lam_init
    k_all = jnp.concatenate([kc, kl], axis=1)
    v_all = jnp.concatenate([vc, vl], axis=1)

    def finish(o):
        B, S = o.shape[:2]
        return (rms_norm(o, subln_g) * (1.0 - lam_init)).reshape(B, S, BRANCH_W)

    y_l = finish(blocked_diff_attention(ql, k_all, v_all, lam))
    y_c = finish(diff_softmax_attend(qc, kc, vc, lam)) if need_ctx else None
    return y_l, y_c


def merge_branches(ys, gate_pre, w_branch, w_out):
    g = jax.nn.sigmoid(gate_pre.reshape(*gate_pre.shape[:-1], N_BRANCH, D_MODEL))
    y = g[..., 0, :] * (ys[0] @ w_branch[0])
    for j in range(1, N_BRANCH):
        y = y + g[..., j, :] * (ys[j] @ w_branch[j])
    return y @ w_out


def moe_ffn(h, w_router, b_router, w_e_gate, w_e_up, w_e_down, w_s_gate, w_s_up, w_s_down):
    shp = h.shape
    t = h.reshape(-1, D_MODEL)
    scores = jax.nn.sigmoid((t @ w_router).astype(jnp.float32))
    _, idx = lax.top_k(scores + b_router.astype(jnp.float32), TOP_K)
    sel = jnp.take_along_axis(scores, idx, axis=-1)
    wts = sel / jnp.sum(sel, axis=-1, keepdims=True) * ROUTED_SCALE
    rows = jnp.arange(t.shape[0])[:, None]
    combine = jnp.zeros((t.shape[0], N_EXPERTS), jnp.float32).at[rows, idx].set(wts).astype(t.dtype)
    out = (jax.nn.silu(t @ w_s_gate) * (t @ w_s_up)) @ w_s_down
    for g0 in range(0, N_EXPERTS, EXPERT_GROUP):
        sl = slice(g0, g0 + EXPERT_GROUP)
        hg = jnp.einsum('td,edf->tef', t, w_e_gate[sl])
        hu = jnp.einsum('td,edf->tef', t, w_e_up[sl])
        act = jax.nn.silu(hg) * hu * combine[:, sl, None]
        out = out + jnp.einsum('tef,efd->td', act, w_e_down[sl])
    return out.reshape(shp)


def trunk_layer(x, xc, c, c_ctx, p, layer_idx, cos, sin, need_ctx):
    lam_init = 0.8 - 0.6 * math.exp(-0.3 * layer_idx)
    m_l = [m[:, None, :] for m in ada_mods(c, p['w_ada'], p['b_ada'])]
    m_c = ada_mods(c_ctx, p['w_ada'], p['b_ada'])
    splits = proj_split_points()
    h_l = modulate(rms_norm(x, p['norm1_g']), m_l[0], m_l[1])
    h_c = modulate(rms_norm(xc, p['norm1_g']), m_c[0], m_c[1])
    P_l = jnp.split(h_l @ p['w_in'], splits, axis=-1)
    P_c = jnp.split(h_c @ p['w_in'], splits, axis=-1)
    yA = rglru_branch(P_c[0:2], P_l[0:2], p['conv_w'], p['conv_b'], p['lru_wa'], p['lru_ba'],
                      p['lru_wx'], p['lru_bx'], p['lru_lam'], need_ctx)
    yB = mlstm_branch(P_c[2:7], P_l[2:7], p['ml_gate_b'], p['ml_norm_g'], need_ctx)
    yC = diff_attn_branch(P_c[7:10], P_l[7:10], p['da_qnorm_g'], p['da_knorm_g'], p['da_lambda'],
                          p['da_subln_g'], lam_init, cos, sin, need_ctx)
    moe_w = (p['w_router'], p['b_router'], p['w_e_gate'], p['w_e_up'], p['w_e_down'],
             p['w_s_gate'], p['w_s_up'], p['w_s_down'])
    x = x + m_l[2] * merge_branches((yA[0], yB[0], yC[0]), P_l[10], p['w_branch'], p['w_out'])
    x = x + m_l[5] * moe_ffn(modulate(rms_norm(x, p['norm2_g']), m_l[3], m_l[4]), *moe_w)
    if need_ctx:
        xc = xc + m_c[2] * merge_branches((yA[1], yB[1], yC[1]), P_c[10], p['w_branch'], p['w_out'])
        xc = xc + m_c[5] * moe_ffn(modulate(rms_norm(xc, p['norm2_g']), m_c[3], m_c[4]), *moe_w)
    return x, xc


def setup_inputs(seed: int = 0) -> dict:
    key = jax.random.key(seed)
    keys = iter(jax.random.split(key, 40))
    L, D = DEPTH, D_MODEL

    def nrm(shape, scale):
        return jax.random.normal(next(keys), shape, jnp.float32) * scale

    u = jax.random.uniform(next(keys), (L, 2, LRU_WIDTH), jnp.float32, 0.9, 0.999)
    a0 = u ** (1.0 / LRU_C)
    ml_gate_b = nrm((L, 2, 2, ML_HEADS), 0.1)
    ml_gate_b = ml_gate_b.at[:, :, 1].add(jnp.linspace(3.0, 6.0, ML_HEADS))
    return {
        'x': nrm((BATCH, SEQ, D), 1.0),
        'c': nrm((BATCH, D), 1.0),
        'ctx': nrm((BATCH, CTX_LEN, D), 1.0),
        'c_ctx': nrm((D,), 1.0),
        'norm1_g': 1.0 + nrm((L, D), 0.02),
        'norm2_g': 1.0 + nrm((L, D), 0.02),
        'w_ada': nrm((L, D, N_MODS * D), 0.5 * D ** -0.5),
        'b_ada': nrm((L, N_MODS * D), 0.02),
        'w_in': nrm((L, D, PROJ_WIDTH), D ** -0.5),
        'conv_w': nrm((L, CONV_WIDTH, LRU_WIDTH), CONV_WIDTH ** -0.5),
        'conv_b': nrm((L, LRU_WIDTH), 0.02),
        'lru_wa': nrm((L, 2, LRU_BLOCKS, LRU_BLOCK, LRU_BLOCK), LRU_BLOCK ** -0.5),
        'lru_ba': nrm((L, 2, LRU_WIDTH), 0.02),
        'lru_wx': nrm((L, 2, LRU_BLOCKS, LRU_BLOCK, LRU_BLOCK), LRU_BLOCK ** -0.5),
        'lru_bx': nrm((L, 2, LRU_WIDTH), 0.02),
        'lru_lam': jnp.log(a0) - jnp.log1p(-a0),
        'ml_gate_b': ml_gate_b,
        'ml_norm_g': 1.0 + nrm((L, ML_HEADS, ML_DV), 0.02),
        'da_qnorm_g': 1.0 + nrm((L, DA_DH), 0.02),
        'da_knorm_g': 1.0 + nrm((L, DA_DH), 0.02),
        'da_lambda': nrm((L, 4, DA_DH), 0.1),
        'da_subln_g': 1.0 + nrm((L, DA_DV), 0.02),
        'w_branch': nrm((L, N_BRANCH, BRANCH_W, D), BRANCH_W ** -0.5),
        'w_out': nrm((L, D, D), D ** -0.5),
        'w_router': nrm((L, D, N_EXPERTS), D ** -0.5),
        'b_router': nrm((L, N_EXPERTS), 0.01),
        'w_e_gate': nrm((L, N_EXPERTS, D, EXPERT_FF), D ** -0.5),
        'w_e_up': nrm((L, N_EXPERTS, D, EXPERT_FF), D ** -0.5),
        'w_e_down': nrm((L, N_EXPERTS, EXPERT_FF, D), EXPERT_FF ** -0.5),
        'w_s_gate': nrm((L, D, SHARED_FF), D ** -0.5),
        'w_s_up': nrm((L, D, SHARED_FF), D ** -0.5),
        'w_s_down': nrm((L, SHARED_FF, D), SHARED_FF ** -0.5),
    }


def reference(x, c, ctx, c_ctx, norm1_g, norm2_g, w_ada, b_ada, w_in, conv_w, conv_b,
              lru_wa, lru_ba, lru_wx, lru_bx, lru_lam, ml_gate_b, ml_norm_g,
              da_qnorm_g, da_knorm_g, da_lambda, da_subln_g, w_branch, w_out,
              w_router, b_router, w_e_gate, w_e_up, w_e_down, w_s_gate, w_s_up, w_s_down):
    rows = x.shape[1] // GRID_W
    cos, sin = axial_rope_tables(rows)
    xc = ctx
    for i in range(DEPTH):
        p = dict(norm1_g=norm1_g[i], norm2_g=norm2_g[i], w_ada=w_ada[i], b_ada=b_ada[i],
                 w_in=w_in[i], conv_w=conv_w[i], conv_b=conv_b[i], lru_wa=lru_wa[i],
                 lru_ba=lru_ba[i], lru_wx=lru_wx[i], lru_bx=lru_bx[i], lru_lam=lru_lam[i],
                 ml_gate_b=ml_gate_b[i], ml_norm_g=ml_norm_g[i], da_qnorm_g=da_qnorm_g[i],
                 da_knorm_g=da_knorm_g[i], da_lambda=da_lambda[i], da_subln_g=da_subln_g[i],
                 w_branch=w_branch[i], w_out=w_out[i], w_router=w_router[i], b_router=b_router[i],
                 w_e_gate=w_e_gate[i], w_e_up=w_e_up[i], w_e_down=w_e_down[i],
                 w_s_gate=w_s_gate[i], w_s_up=w_s_up[i], w_s_down=w_s_down[i])
        x, xc = trunk_layer(x, xc, c, c_ctx, p, i, cos, sin, need_ctx=(i < DEPTH - 1))
    return x
```

```python
import functools
import math

import jax
import jax.numpy as jnp
from jax import lax
from jax.experimental import pallas as pl
from jax.experimental.pallas import tpu as pltpu

F32 = jnp.float32
BF16 = jnp.bfloat16

NORM_EPS = 1e-6
N_MODS = 6
GRID_W = 64
ROPE_THETA = 10000.0
LRU_BLOCK = 128
LRU_C = 8.0
ML_HEADS = 4
ML_DQK = 128
ML_DV = 256
DA_HEADS = 8
DA_DH = 64
DA_DV = 128
N_EXPERTS = 64
TOP_K = 6
EXPERT_FF = 256
ROUTED_SCALE = 2.5
EXPERT_GROUP = 8

COL_XA, COL_GA = 0, 1024
COL_MLQ, COL_MLK, COL_MLV, COL_MLO = 2048, 2560, 3072, 4096
COL_DAQ, COL_DAK, COL_DAV = 5120, 6144, 7168
COL_GATE = 8192
PROJ_MAIN = 11264

LANES = 128
VMEM_LIMIT = 56 * 1024 * 1024
NEG_BIG = -1e30


def _cparams(sem):
    return pltpu.CompilerParams(dimension_semantics=sem, vmem_limit_bytes=VMEM_LIMIT)


def _dot(a, b):
    return jnp.dot(a, b, preferred_element_type=F32)


def _dot_nt(a, b):
    return lax.dot_general(a, b, (((1,), (1,)), ((), ())), preferred_element_type=F32)


def _dot_tn(a, b):
    return lax.dot_general(a, b, (((0,), (0,)), ((), ())), preferred_element_type=F32)


def _sigmoid(x):
    return 1.0 / (1.0 + jnp.exp(-x))


def _silu(x):
    return x * _sigmoid(x)


def _ada_kernel(c_ref, w_ref, b_ref, o_ref):
    c = c_ref[...]
    s = _silu(c).astype(BF16)
    o_ref[...] = _dot(s, w_ref[...].astype(BF16)) + b_ref[...]


def ada_mods(cond, w_ada, b_ada):
    nb, d = cond.shape
    n = w_ada.shape[1]
    tn = 768
    return pl.pallas_call(
        _ada_kernel,
        grid=(n // tn,),
        in_specs=[pl.BlockSpec((nb, d), lambda j: (0, 0)),
                  pl.BlockSpec((d, tn), lambda j: (0, j)),
                  pl.BlockSpec((1, tn), lambda j: (0, j))],
        out_specs=pl.BlockSpec((nb, tn), lambda j: (0, j)),
        out_shape=jax.ShapeDtypeStruct((nb, n), F32),
        compiler_params=_cparams(("parallel",)),
        name="ada_mods",
    )(cond, w_ada, b_ada.reshape(1, n))


def _norm_modulate(x, g, shift, scale):
    y = x * lax.rsqrt(jnp.mean(x * x, axis=-1, keepdims=True) + NORM_EPS) * g
    return y * (1.0 + scale) + shift


def _inproj_kernel(x_ref, mods_ref, g_ref, w_ref, wg_ref, p_ref, gate_ref, h_scr, *,
                   ctx_len, ctx_row, d, tm, rows):
    b = pl.program_id(0)
    i = pl.program_id(1)
    j = pl.program_id(2)

    @pl.when(j == 0)
    def _():
        g = g_ref[...]
        sh = mods_ref[pl.ds(b, 1), pl.ds(0, d)]
        sc = mods_ref[pl.ds(b, 1), pl.ds(d, d)]
        h_scr[...] = _norm_modulate(x_ref[0], g, sh, sc).astype(BF16)

        @pl.when(i == 0)
        def _():
            shc = mods_ref[pl.ds(ctx_row, 1), pl.ds(0, d)]
            scc = mods_ref[pl.ds(ctx_row, 1), pl.ds(d, d)]
            h_scr[pl.ds(0, ctx_len), :] = _norm_modulate(
                x_ref[0, pl.ds(0, ctx_len), :], g, shc, scc).astype(BF16)

        gate_ref[0] = _dot(h_scr[...], wg_ref[...])

    w = w_ref[...]
    for r in range(0, tm, rows):
        p_ref[0, r:r + rows, :] = _dot(h_scr[r:r + rows, :], w).astype(BF16)


def in_projection(x_all, mods, norm_g, w_main, w_gate, *, ctx_len, tm, tn):
    bsz, s_all, d = x_all.shape
    n = w_main.shape[1]
    assert s_all % tm == 0 and n % tn == 0 and tm >= ctx_len
    rows = math.gcd(tm, 384)
    kern = functools.partial(_inproj_kernel, ctx_len=ctx_len, ctx_row=bsz, d=d, tm=tm, rows=rows)
    return pl.pallas_call(
        kern,
        grid=(bsz, s_all // tm, n // tn),
        in_specs=[pl.BlockSpec((1, tm, d), lambda b, i, j: (b, i, 0)),
                  pl.BlockSpec(mods.shape, lambda b, i, j: (0, 0)),
                  pl.BlockSpec((1, d), lambda b, i, j: (0, 0)),
                  pl.BlockSpec((d, tn), lambda b, i, j: (0, j)),
                  pl.BlockSpec((d, LANES), lambda b, i, j: (0, 0))],
        out_specs=[pl.BlockSpec((1, tm, tn), lambda b, i, j: (b, i, j)),
                   pl.BlockSpec((1, tm, LANES), lambda b, i, j: (b, i, 0))],
        out_shape=[jax.ShapeDtypeStruct((bsz, s_all, n), BF16),
                   jax.ShapeDtypeStruct((bsz, s_all, LANES), F32)],
        scratch_shapes=[pltpu.VMEM((tm, d), BF16)],
        compiler_params=_cparams(("parallel", "parallel", "arbitrary")),
        name="in_projection",
    )(x_all, mods, norm_g.reshape(1, d), w_main, w_gate)


def _lru_chunk(g, *, reverse, n_ctx, n_all):
    if not reverse:
        return g
    return jnp.where(g < n_ctx, n_ctx - 1 - g, n_all - 1 - (g - n_ctx))


def _lru_kernel(*refs, reverse, ts, n_ctx, n_all):
    if reverse:
        (xa_ref, prev_ref, next_ref, ga_ref, hf_ref, cw_ref, cb_ref, wa_ref, ba_ref, wx_ref,
         bx_ref, lam_ref, out_ref, a_scr, b_scr, h_scr) = refs
    else:
        (xa_ref, prev_ref, next_ref, cw_ref, cb_ref, wa_ref, ba_ref, wx_ref, bx_ref, lam_ref,
         out_ref, a_scr, b_scr, h_scr) = refs
    g = pl.program_id(1)
    c = _lru_chunk(g, reverse=reverse, n_ctx=n_ctx, n_all=n_all)
    bsz = xa_ref.shape[1]

    @pl.when(g == 0)
    def _():
        h_scr[...] = jnp.zeros_like(h_scr)

    prev_ok = jnp.where((c == 0) | (c == n_ctx), 0.0, 1.0).astype(F32)
    next_ok = jnp.where((c == n_ctx - 1) | (c == n_all - 1), 0.0, 1.0).astype(F32)
    xa = xa_ref[...].astype(F32)
    ext = jnp.concatenate([prev_ref[...].astype(F32) * prev_ok, xa,
                           next_ref[...].astype(F32) * next_ok], axis=0)
    cw = cw_ref[...]
    u = cb_ref[...][None]
    for k in range(4):
        u = u + cw[k:k + 1][None] * ext[k:k + ts]
    u2 = u.reshape(ts * bsz, LANES)
    ub = u2.astype(BF16)
    r = _sigmoid(_dot(ub, wa_ref[0].astype(BF16)) + ba_ref[...])
    ig = _sigmoid(_dot(ub, wx_ref[0].astype(BF16)) + bx_ref[...])
    nl = -lam_ref[...]
    softplus = jnp.maximum(nl, 0.0) + jnp.log(1.0 + jnp.exp(-jnp.abs(nl)))
    log_a = (-LRU_C) * r * softplus
    a = jnp.exp(log_a)
    bc = jnp.sqrt(1.0 - a * a) * (ig * u2)
    a_scr[...] = a.reshape(ts, bsz, LANES)
    b_scr[...] = bc.reshape(ts, bsz, LANES)

    def step(t, h):
        tt = (ts - 1 - t) if reverse else t
        h = a_scr[tt] * h + b_scr[tt]
        b_scr[tt] = h
        return h

    h_scr[...] = lax.fori_loop(0, ts, step, h_scr[...], unroll=8)

    if reverse:
        hsum = hf_ref[...].astype(F32) + b_scr[...]
        out_ref[...] = (jax.nn.gelu(ga_ref[...].astype(F32)) * hsum).astype(out_ref.dtype)
    else:
        out_ref[...] = b_scr[...].astype(out_ref.dtype)


def lru_direction(xa_tm, ga_tm, hf_tm, conv_w, conv_b, wa, ba, wx, bx, lam, *, reverse, ctx_len, ts):
    s_all, bsz, w = xa_tm.shape
    assert ctx_len % ts == 0 and s_all % ts == 0 and ts % 2 == 0
    n_ctx, n_all = ctx_len // ts, s_all // ts
    cm = functools.partial(_lru_chunk, reverse=reverse, n_ctx=n_ctx, n_all=n_all)
    blk = pl.BlockSpec((ts, bsz, LANES), lambda j, g: (cm(g), 0, j))
    prev = pl.BlockSpec((2, bsz, LANES), lambda j, g: (jnp.maximum(cm(g) * (ts // 2) - 1, 0), 0, j))
    nxt = pl.BlockSpec((1, bsz, LANES), lambda j, g: (jnp.minimum((cm(g) + 1) * ts, s_all - 1), 0, j))
    vec = pl.BlockSpec((1, LANES), lambda j, g: (0, j))
    mat = pl.BlockSpec((1, LRU_BLOCK, LRU_BLOCK), lambda j, g: (j, 0, 0))
    params = [conv_w, conv_b.reshape(1, w), wa, ba.reshape(1, w), wx, bx.reshape(1, w), lam.reshape(1, w)]
    pspecs = [pl.BlockSpec((4, LANES), lambda j, g: (0, j)), vec, mat, vec, mat, vec, vec]
    if reverse:
        ins, specs = [xa_tm, xa_tm, xa_tm, ga_tm, hf_tm], [blk, prev, nxt, blk, blk]
    else:
        ins, specs = [xa_tm, xa_tm, xa_tm], [blk, prev, nxt]
    return pl.pallas_call(
        functools.partial(_lru_kernel, reverse=reverse, ts=ts, n_ctx=n_ctx, n_all=n_all),
        grid=(w // LANES, n_all),
        in_specs=specs + pspecs,
        out_specs=blk,
        out_shape=jax.ShapeDtypeStruct((s_all, bsz, w), BF16),
        scratch_shapes=[pltpu.VMEM((ts, bsz, LANES), F32), pltpu.VMEM((ts, bsz, LANES), F32),
                        pltpu.VMEM((bsz, LANES), F32)],
        compiler_params=_cparams(("parallel", "arbitrary")),
        name="lru_bwd" if reverse else "lru_fwd",
    )(*ins, *params)


def _ml_chunk(g, *, reverse, n_ctx, n_all):
    if not reverse:
        return g
    return jnp.where(g < n_ctx, n_ctx - 1 - g, n_all - 1 - (g - n_ctx))


def _mlstm_kernel(q_ref, k_ref, v_ref, g_ref, gb_ref, o_ref, c_scr, m_scr, *, reverse, d_idx, chunk):
    g = pl.program_id(1)
    L = chunk
    ext = ML_DV + LANES

    @pl.when(g == 0)
    def _():
        c_scr[...] = jnp.zeros_like(c_scr)
        m_scr[...] = jnp.zeros_like(m_scr)

    gates = g_ref[0] + gb_ref[...]
    logf = jnp.minimum(gates, 0.0) - jnp.log(1.0 + jnp.exp(-jnp.abs(gates)))
    row = lax.broadcasted_iota(jnp.int32, (L, L), 0)
    col = lax.broadcasted_iota(jnp.int32, (L, L), 1)
    keep = (col >= row) if reverse else (col <= row)
    tri = jnp.where(keep, 1.0, 0.0).astype(BF16)
    hi = logf.astype(BF16)
    lo = (logf - hi.astype(F32)).astype(BF16)
    cum = _dot(tri, hi) + _dot(tri, lo)
    cum_t = cum.T
    gates_t = gates.T
    tot_all = jnp.sum(logf, axis=0, keepdims=True)
    ones_col = jnp.where(lax.broadcasted_iota(jnp.int32, (L, LANES), 1) == 0, 1.0, 0.0).astype(BF16)
    scale = ML_DQK ** -0.5

    for h in range(ML_HEADS):
        icol = d_idx * 2 * ML_HEADS + h
        fcol = icol + ML_HEADS
        bc = cum[:, fcol:fcol + 1]
        br = cum_t[fcol:fcol + 1, :]
        ic = gates[:, icol:icol + 1]
        ir = gates_t[icol:icol + 1, :]
        btot = tot_all[:, fcol:fcol + 1]
        m_old = m_scr[h][:, 0:1]
        q = q_ref[0, :, h * ML_DQK:(h + 1) * ML_DQK]
        k = k_ref[0, :, h * ML_DQK:(h + 1) * ML_DQK]
        v = v_ref[0, :, h * ML_DV:(h + 1) * ML_DV]
        v_ext = jnp.concatenate([v, ones_col], axis=-1)

        d_intra = jnp.where(keep, bc - br + ir, NEG_BIG)
        d_inter = bc + m_old
        m_row = jnp.maximum(d_inter, jnp.max(d_intra, axis=-1, keepdims=True))
        w_intra = jnp.exp(d_intra - m_row)
        w_inter = jnp.exp(d_inter - m_row)
        s = _dot_nt(q, k) * scale * w_intra
        c_old = c_scr[h]
        num = _dot(s.astype(BF16), v_ext) + _dot(q, c_old.astype(BF16)) * w_inter
        den = num[:, ML_DV:ML_DV + 1]
        hout = num[:, 0:ML_DV] / jnp.maximum(jnp.abs(den), jnp.exp(-m_row))
        o_ref[0, :, h * ML_DV:(h + 1) * ML_DV] = hout.astype(o_ref.dtype)

        g_end = btot - bc + ic
        m_new = jnp.maximum(btot + m_old, jnp.max(g_end, axis=0, keepdims=True))
        w_s = jnp.exp(g_end - m_new)
        w_c = jnp.exp(btot + m_old - m_new)
        kw = (k.astype(F32) * (scale * w_s)).astype(BF16)
        c_scr[h] = w_c * c_old + _dot_tn(kw, v_ext)
        m_scr[h] = jnp.broadcast_to(m_new, (1, LANES))


def mlstm_direction(proj, gates, gate_b, *, reverse, d_idx, ctx_len, chunk):
    bsz, s_all, _ = proj.shape
    n_ctx, n_all = ctx_len // chunk, s_all // chunk
    cm = functools.partial(_ml_chunk, reverse=reverse, n_ctx=n_ctx, n_all=n_all)
    hq = ML_HEADS * ML_DQK
    hv = ML_HEADS * ML_DV
    gb = jnp.zeros((1, LANES), F32).at[0, :gate_b.size].set(gate_b.reshape(-1))
    return pl.pallas_call(
        functools.partial(_mlstm_kernel, reverse=reverse, d_idx=d_idx, chunk=chunk),
        grid=(bsz, n_all),
        in_specs=[pl.BlockSpec((1, chunk, hq), lambda b, g: (b, cm(g), COL_MLQ // hq)),
                  pl.BlockSpec((1, chunk, hq), lambda b, g: (b, cm(g), COL_MLK // hq)),
                  pl.BlockSpec((1, chunk, hv), lambda b, g: (b, cm(g), COL_MLV // hv)),
                  pl.BlockSpec((1, chunk, LANES), lambda b, g: (b, cm(g), 0)),
                  pl.BlockSpec((1, LANES), lambda b, g: (0, 0))],
        out_specs=pl.BlockSpec((1, chunk, hv), lambda b, g: (b, cm(g), 0)),
        out_shape=jax.ShapeDtypeStruct((bsz, s_all, hv), BF16),
        scratch_shapes=[pltpu.VMEM((ML_HEADS, ML_DQK, ML_DV + LANES), F32),
                        pltpu.VMEM((ML_HEADS, 1, LANES), F32)],
        compiler_params=_cparams(("parallel", "arbitrary")),
        name="mlstm_bwd" if reverse else "mlstm_fwd",
    )(proj, proj, proj, gates, gb)


def _qkprep_kernel(q_ref, k_ref, cos_ref, sin_ref, qg_ref, kg_ref, qo_ref, ko_ref):
    lane_r = lax.broadcasted_iota(jnp.int32, (LANES, LANES), 0)
    lane_c = lax.broadcasted_iota(jnp.int32, (LANES, LANES), 1)
    group = jnp.where((lane_r // DA_DH) == (lane_c // DA_DH), 1.0, 0.0).astype(BF16)
    swap = jnp.where(lane_r == (lane_c ^ 16), 1.0, 0.0).astype(BF16)
    cos = cos_ref[...]
    sin = sin_ref[...]

    def prep(x_ref, gain, o_ref, out_scale):
        for h in range(DA_HEADS):
            t = x_ref[0, :, h * LANES:(h + 1) * LANES].astype(F32)
            ss = _dot((t * t).astype(BF16), group)
            tn = t * lax.rsqrt(ss * (1.0 / DA_DH) + NORM_EPS) * gain
            rot = tn * cos + _dot(tn.astype(BF16), swap) * sin
            o_ref[0, :, h * LANES:(h + 1) * LANES] = (rot * out_scale).astype(o_ref.dtype)

    prep(q_ref, qg_ref[...], qo_ref, DA_DH ** -0.5)
    prep(k_ref, kg_ref[...], ko_ref, 1.0)


def qk_prep(proj, cos_tab, sin_tab, qn_g, kn_g, *, ts):
    bsz, s_all, _ = proj.shape
    w = DA_HEADS * LANES
    qg = jnp.tile(qn_g.astype(F32), 2).reshape(1, LANES)
    kg = jnp.tile(kn_g.astype(F32), 2).reshape(1, LANES)
    spec = lambda cb: pl.BlockSpec((1, ts, w), lambda b, i: (b, i, cb))
    tab = pl.BlockSpec((ts, LANES), lambda b, i: (i, 0))
    vec = pl.BlockSpec((1, LANES), lambda b, i: (0, 0))
    return pl.pallas_call(
        _qkprep_kernel,
        grid=(bsz, s_all // ts),
        in_specs=[spec(COL_DAQ // w), spec(COL_DAK // w), tab, tab, vec, vec],
        out_specs=[spec(0), spec(0)],
        out_shape=[jax.ShapeDtypeStruct((bsz, s_all, w), BF16)] * 2,
        compiler_params=_cparams(("parallel", "parallel")),
        name="qk_prep",
    )(proj, proj, cos_tab, sin_tab, qg, kg)


def _softmax_rows(s):
    m = jnp.max(s, axis=-1, keepdims=True)
    e = jnp.exp(s - m)
    return e / jnp.sum(e, axis=-1, keepdims=True)


def _attn_kernel(lam_ref, q_ref, k_ref, v_ref, sg_ref, o_ref, *, out_scale):
    lam = lam_ref[0]
    q = q_ref[0]
    k = k_ref[0]
    lane = lax.broadcasted_iota(jnp.int32, q.shape, 1)
    zero = jnp.zeros_like(q)
    q0 = jnp.where(lane < DA_DH, q, zero)
    q1 = jnp.where(lane >= DA_DH, q, zero)
    p0 = _softmax_rows(_dot_nt(q0, k))
    p1 = _softmax_rows(_dot_nt(q1, k))
    a = (p0 - lam * p1).astype(BF16)
    o = _dot(a, v_ref[0])
    o = o * lax.rsqrt(jnp.mean(o * o, axis=-1, keepdims=True) + NORM_EPS) * sg_ref[...]
    o_ref[0] = (o * out_scale).astype(o_ref.dtype)


def diff_attention(qn, kn, proj, lam, subln_g, *, q_start, q_len, k_len, tq, out_scale):
    bsz = qn.shape[0]
    assert q_start % tq == 0 and q_len % tq == 0
    q0 = q_start // tq
    vb = COL_DAV // LANES
    return pl.pallas_call(
        functools.partial(_attn_kernel, out_scale=out_scale),
        grid=(bsz, DA_HEADS, q_len // tq),
        in_specs=[pl.BlockSpec(memory_space=pltpu.SMEM),
                  pl.BlockSpec((1, tq, LANES), lambda b, h, i: (b, q0 + i, h)),
                  pl.BlockSpec((1, k_len, LANES), lambda b, h, i: (b, 0, h)),
                  pl.BlockSpec((1, k_len, LANES), lambda b, h, i: (b, 0, vb + h)),
                  pl.BlockSpec((1, LANES), lambda b, h, i: (0, 0))],
        out_specs=pl.BlockSpec((1, tq, LANES), lambda b, h, i: (b, i, h)),
        out_shape=jax.ShapeDtypeStruct((bsz, q_len, DA_HEADS * DA_DV), BF16),
        compiler_params=_cparams(("parallel", "parallel", "arbitrary")),
        name="diff_attention",
    )(lam, qn, kn, proj, subln_g.reshape(1, LANES).astype(F32))


def _merge_kernel(x_ref, mods_ref, ya_ref, hf_ref, hb_ref, o_ref, yc_ref, g0_ref, g1_ref, g2_ref,
                  mlg_ref, wb_ref, wo_ref, out_ref, *, n_ctx_tiles, ctx_row, d):
    b = pl.program_id(0)
    i = pl.program_id(1)
    row = jnp.where(i < n_ctx_tiles, ctx_row, b)
    gate_res = mods_ref[pl.ds(row, 1), pl.ds(2 * d, d)]

    hsum = hf_ref[0].astype(F32) + hb_ref[0].astype(F32)
    parts = []
    for h in range(ML_HEADS):
        t = hsum[:, h * ML_DV:(h + 1) * ML_DV]
        parts.append(t * lax.rsqrt(jnp.mean(t * t, axis=-1, keepdims=True) + NORM_EPS))
    yb = jnp.concatenate(parts, axis=-1) * mlg_ref[...] * _sigmoid(o_ref[0].astype(F32))

    y = _sigmoid(g0_ref[0].astype(F32)) * _dot(ya_ref[0], wb_ref[0])
    y = y + _sigmoid(g1_ref[0].astype(F32)) * _dot(yb.astype(BF16), wb_ref[1])
    y = y + _sigmoid(g2_ref[0].astype(F32)) * _dot(yc_ref[0], wb_ref[2])
    out_ref[0] = x_ref[0] + gate_res * _dot(y.astype(BF16), wo_ref[...])


def merge_branches(x_all, mods, ya, hf, hb, proj, yc, ml_norm_g, w_branch, w_out, *,
                   ctx_len, row_start, tm):
    bsz, s_all, d = x_all.shape
    r0 = row_start // tm
    n_tiles = (s_all - row_start) // tm
    n_ctx_tiles = (ctx_len - row_start) // tm if row_start < ctx_len else 0
    tok = lambda cb: pl.BlockSpec((1, tm, d), lambda b, i: (b, r0 + i, cb))
    return pl.pallas_call(
        functools.partial(_merge_kernel, n_ctx_tiles=n_ctx_tiles, ctx_row=bsz, d=d),
        grid=(bsz, n_tiles),
        in_specs=[tok(0),
                  pl.BlockSpec(mods.shape, lambda b, i: (0, 0)),
                  tok(0), tok(0), tok(0), tok(COL_MLO // d),
                  pl.BlockSpec((1, tm, d), lambda b, i: (b, i, 0)),
                  tok(COL_GATE // d), tok(COL_GATE // d + 1), tok(COL_GATE // d + 2),
                  pl.BlockSpec((1, d), lambda b, i: (0, 0)),
                  pl.BlockSpec((3, d, d), lambda b, i: (0, 0, 0)),
                  pl.BlockSpec((d, d), lambda b, i: (0, 0))],
        out_specs=pl.BlockSpec((1, tm, d), lambda b, i: (b, i, 0)),
        out_shape=jax.ShapeDtypeStruct((bsz, s_all - row_start, d), F32),
        compiler_params=_cparams(("parallel", "parallel")),
        name="merge_branches",
    )(x_all, mods, ya, hf, hb, proj, yc, proj, proj, proj, ml_norm_g.reshape(1, d).astype(F32),
      w_branch, w_out)


def _route(scores, bias):
    lane = lax.broadcasted_iota(jnp.int32, scores.shape, 1)
    valid = lane < N_EXPERTS
    work = jnp.where(valid, scores + bias, NEG_BIG)
    sel = jnp.zeros_like(scores)
    for _ in range(TOP_K):
        m = jnp.max(work, axis=-1, keepdims=True)
        first = jnp.min(jnp.where(work == m, lane, LANES), axis=-1, keepdims=True)
        pick = lane == first
        sel = jnp.where(pick, scores, sel)
        work = jnp.where(pick, NEG_BIG, work)
    return sel / jnp.sum(sel, axis=-1, keepdims=True) * ROUTED_SCALE


def _moe_kernel(x_ref, mods_ref, g_ref, wr_ref, br_ref, wsg_ref, wsu_ref, wsd_ref, weg_ref, weu_ref,
                wed_ref, out_ref, h_scr, comb_scr, acc_scr, *, n_ctx_tiles, ctx_row, d):
    b = pl.program_id(0)
    i = pl.program_id(1)
    eg = pl.program_id(2)
    row = jnp.where(i < n_ctx_tiles, ctx_row, b)

    @pl.when(eg == 0)
    def _():
        sh = mods_ref[pl.ds(row, 1), pl.ds(3 * d, d)]
        sc = mods_ref[pl.ds(row, 1), pl.ds(4 * d, d)]
        h = _norm_modulate(x_ref[0], g_ref[...], sh, sc).astype(BF16)
        h_scr[...] = h
        scores = _sigmoid(_dot(h, wr_ref[...]))
        comb_scr[...] = _route(scores, br_ref[...])
        act = _silu(_dot(h, wsg_ref[...])) * _dot(h, wsu_ref[...])
        acc_scr[...] = _dot(act.astype(BF16), wsd_ref[...])

    h = h_scr[...]
    comb = comb_scr[...]
    lane = lax.broadcasted_iota(jnp.int32, comb.shape, 1)
    acc = acc_scr[...]
    for e in range(EXPERT_GROUP):
        ce = jnp.sum(jnp.where(lane == eg * EXPERT_GROUP + e, comb, 0.0), axis=-1, keepdims=True)
        act = _silu(_dot(h, weg_ref[e])) * _dot(h, weu_ref[e]) * ce
        acc = acc + _dot(act.astype(BF16), wed_ref[e])
    acc_scr[...] = acc

    @pl.when(eg == pl.num_programs(2) - 1)
    def _():
        gate_res = mods_ref[pl.ds(row, 1), pl.ds(5 * d, d)]
        out_ref[0] = x_ref[0] + gate_res * acc


def moe_ffn(x1, mods, norm_g, w_router, b_router, w_s_gate, w_s_up, w_s_down, w_e_gate, w_e_up,
            w_e_down, *, ctx_rows, tm):
    bsz, s1, d = x1.shape
    n_tiles = s1 // tm
    n_ctx_tiles = ctx_rows // tm
    ff = w_e_gate.shape[2]
    sff = w_s_gate.shape[1]
    wr = jnp.zeros((d, LANES), BF16).at[:, :N_EXPERTS].set(w_router.astype(BF16))
    br = jnp.zeros((1, LANES), F32).at[0, :N_EXPERTS].set(b_router.astype(F32))
    const = lambda shape: pl.BlockSpec(shape, lambda b, i, e: (0,) * len(shape))
    return pl.pallas_call(
        functools.partial(_moe_kernel, n_ctx_tiles=n_ctx_tiles, ctx_row=bsz, d=d),
        grid=(bsz, n_tiles, N_EXPERTS // EXPERT_GROUP),
        in_specs=[pl.BlockSpec((1, tm, d), lambda b, i, e: (b, i, 0)),
                  const(mods.shape), const((1, d)), const((d, LANES)), const((1, LANES)),
                  const((d, sff)), const((d, sff)), const((sff, d)),
                  pl.BlockSpec((EXPERT_GROUP, d, ff), lambda b, i, e: (e, 0, 0)),
                  pl.BlockSpec((EXPERT_GROUP, d, ff), lambda b, i, e: (e, 0, 0)),
                  pl.BlockSpec((EXPERT_GROUP, ff, d), lambda b, i, e: (e, 0, 0))],
        out_specs=pl.BlockSpec((1, tm, d), lambda b, i, e: (b, i, 0)),
        out_shape=jax.ShapeDtypeStruct((bsz, s1, d), F32),
        scratch_shapes=[pltpu.VMEM((tm, d), BF16), pltpu.VMEM((tm, LANES), F32),
                        pltpu.VMEM((tm, d), F32)],
        compiler_params=_cparams(("parallel", "parallel", "arbitrary")),
        name="moe_ffn",
    )(x1, mods, norm_g.reshape(1, d), wr, br, w_s_gate, w_s_up, w_s_down, w_e_gate, w_e_up, w_e_down)


def _reorder_w_in(w_in):
    d = w_in.shape[0]
    gate0 = 2 * d + 2 * ML_HEADS * ML_DQK + 2 * ML_HEADS * ML_DV
    ngate = 4 * ML_HEADS
    main = jnp.concatenate([w_in[:, :gate0], w_in[:, gate0 + ngate:]], axis=1).astype(BF16)
    wg = jnp.zeros((d, LANES), BF16).at[:, :ngate].set(w_in[:, gate0:gate0 + ngate].astype(BF16))
    return main, wg


def _rope_tables(ctx_len, seq):
    nf = DA_DH // 4
    freqs = jnp.power(ROPE_THETA, -jnp.arange(nf, dtype=F32) / nf)
    pos = jnp.arange(seq)
    rowcol = jnp.stack([(pos // GRID_W).astype(F32), (pos % GRID_W).astype(F32)], axis=-1)
    lane = jnp.arange(LANES)
    axis = (lane % DA_DH) // (DA_DH // 2)
    half = (lane % (DA_DH // 2)) // nf
    ang = rowcol[:, axis] * freqs[lane % nf][None, :]
    cos = jnp.cos(ang)
    sin = jnp.sin(ang) * jnp.where(half == 0, -1.0, 1.0)[None, :]
    cos = jnp.concatenate([jnp.ones((ctx_len, LANES), F32), cos], axis=0)
    sin = jnp.concatenate([jnp.zeros((ctx_len, LANES), F32), sin], axis=0)
    return cos, sin


def _tiles(s_all, ctx_len):
    half = s_all // 2
    tm_in = half if (half % LANES == 0 and half >= ctx_len) else s_all
    return tm_in


def kernel(x, c, ctx, c_ctx, norm1_g, norm2_g, w_ada, b_ada, w_in, conv_w, conv_b, lru_wa, lru_ba,
           lru_wx, lru_bx, lru_lam, ml_gate_b, ml_norm_g, da_qnorm_g, da_knorm_g, da_lambda,
           da_subln_g, w_branch, w_out, w_router, b_router, w_e_gate, w_e_up, w_e_down, w_s_gate,
           w_s_up, w_s_down):
    bsz, seq, d = x.shape
    ctx_len = ctx.shape[1]
    depth = w_in.shape[0]
    s_all = ctx_len + seq
    nb = -(-(bsz + 1) // 8) * 8
    cond = jnp.zeros((nb, d), F32).at[:bsz].set(c).at[bsz].set(c_ctx)
    cos_tab, sin_tab = _rope_tables(ctx_len, seq)
    x_all = jnp.concatenate([ctx, x], axis=1)
    tm_in = _tiles(s_all, ctx_len)
    lru_ts = 128
    ml_chunk = 256
    tq = 256
    tm = 256

    for l in range(depth):
        need_ctx = l < depth - 1
        lam_init = 0.8 - 0.6 * math.exp(-0.3 * l)
        mods = ada_mods(cond, w_ada[l], b_ada[l])
        w_main, w_gate = _reorder_w_in(w_in[l])
        proj, gates = in_projection(x_all, mods, norm1_g[l], w_main, w_gate,
                                    ctx_len=ctx_len, tm=tm_in, tn=PROJ_MAIN // 4)

        xa_tm = jnp.transpose(proj[:, :, COL_XA:COL_XA + d], (1, 0, 2))
        ga_tm = jnp.transpose(proj[:, :, COL_GA:COL_GA + d], (1, 0, 2))
        lru_p = lambda k: (conv_w[l], conv_b[l], lru_wa[l, k], lru_ba[l, k], lru_wx[l, k],
                           lru_bx[l, k], lru_lam[l, k])
        hf_tm = lru_direction(xa_tm, None, None, *lru_p(0), reverse=False, ctx_len=ctx_len, ts=lru_ts)
        ya_tm = lru_direction(xa_tm, ga_tm, hf_tm, *lru_p(1), reverse=True, ctx_len=ctx_len, ts=lru_ts)
        ya = jnp.transpose(ya_tm, (1, 0, 2))

        hf = mlstm_direction(proj, gates, ml_gate_b[l], reverse=False, d_idx=0, ctx_len=ctx_len,
                             chunk=ml_chunk)
        hb = mlstm_direction(proj, gates, ml_gate_b[l], reverse=True, d_idx=1, ctx_len=ctx_len,
                             chunk=ml_chunk)

        qn, kn = qk_prep(proj, cos_tab, sin_tab, da_qnorm_g[l], da_knorm_g[l], ts=tq)
        lv = da_lambda[l].astype(F32)
        lam = (jnp.exp(jnp.sum(lv[0] * lv[1])) - jnp.exp(jnp.sum(lv[2] * lv[3])) + lam_init).reshape(1)
        yc = diff_attention(qn, kn, proj, lam, da_subln_g[l], q_start=ctx_len, q_len=seq,
                            k_len=s_all, tq=tq, out_scale=1.0 - lam_init)
        if need_ctx:
            yc_c = diff_attention(qn, kn, proj, lam, da_subln_g[l], q_start=0, q_len=ctx_len,
                                  k_len=ctx_len, tq=tq, out_scale=1.0 - lam_init)
            yc = jnp.concatenate([yc_c, yc], axis=1)

        row_start = 0 if need_ctx else ctx_len
        x1 = merge_branches(x_all, mods, ya, hf, hb, proj, yc, ml_norm_g[l], w_branch[l].astype(BF16),
                            w_out[l].astype(BF16), ctx_len=ctx_len, row_start=row_start, tm=tm)
        x_all = moe_ffn(x1, mods, norm2_g[l], w_router[l], b_router[l], w_s_gate[l].astype(BF16),
                        w_s_up[l].astype(BF16), w_s_down[l].astype(BF16), w_e_gate[l].astype(BF16),
                        w_e_up[l].astype(BF16), w_e_down[l].astype(BF16),
                        ctx_rows=ctx_len - row_start, tm=tm)
    return x_all
```
